```python
import math
import jax, jax.numpy as jnp
from jax import lax
import numpy as np

D_MODEL = 4096
BATCH = 4
SEQ = 2048
DEPTH = 2

GRID_W = 64
CTX_LEN = 256
N_MIXERS = 2
N_HY = (DEPTH + 1) // 2
N_NA = DEPTH // 2
HY_ORDER = 2
HY_EMB = 33
HY_BANDS = (HY_EMB - 1) // 2
HY_FILT = 64
HY_SHORT = 3
HY_FAST_DECAY = 0.3
HY_SLOW_DECAY = 1.5
HY_TARGET = 1e-2
HY_SHIFT = 0.05
NA_HEAD_DIM = 128
NA_HEADS = D_MODEL // NA_HEAD_DIM
WIN_H = 8
WIN_W = 16
PEER_HEADS = 8
PEER_KEYS = 128
PEER_EXPERTS = PEER_KEYS * PEER_KEYS
PEER_DKEY = 256
PEER_TOPK = 16
PEER_BLOCK = 128
EPS = 1e-6

kernel_name = 'hybrid_hyena_natten_peer_dit'


def rmsnorm(x, g):
    xf = x.astype(jnp.float32)
    y = xf * lax.rsqrt(jnp.mean(xf * xf, axis=-1, keepdims=True) + EPS)
    return (y * g.astype(jnp.float32)).astype(x.dtype)


def modulate(h, shift, scale):
    return h * (1 + scale) + shift


def ada_mod(cvec, w, b):
    m = jax.nn.silu(cvec) @ w + b
    return jnp.split(m[..., None, :], 6, axis=-1)


def short_conv(u, w, b):
    L = u.shape[1]
    p = HY_SHORT // 2
    up = jnp.pad(u, ((0, 0), (p, HY_SHORT - 1 - p), (0, 0)))
    out = b
    for tap in range(HY_SHORT):
        out = out + up[:, tap:tap + L] * w[tap]
    return out


def hyena_filters(L, w1, b1, w2, b2, w3, b3, w4, freq):
    f32 = jnp.float32
    t = jnp.linspace(0.0, 1.0, L, dtype=f32)[:, None]
    wpos = (2.0 * math.pi / L) * jnp.arange(L, dtype=f32)[:, None]
    bands = jnp.linspace(1e-4, HY_BANDS - 1, HY_BANDS, dtype=f32)[None, :]
    z = jnp.concatenate([t, jnp.cos(bands * wpos), -jnp.sin(bands * wpos)], axis=-1)
    fr = freq.astype(f32)
    h = jnp.sin(fr * (z @ w1.astype(f32) + b1.astype(f32)))
    h = jnp.sin(fr * (h @ w2.astype(f32) + b2.astype(f32)))
    h = jnp.sin(fr * (h @ w3.astype(f32) + b3.astype(f32)))
    h = (h @ w4.astype(f32)).reshape(L, 2, HY_ORDER, D_MODEL)
    max_decay = math.log(HY_TARGET) / HY_FAST_DECAY
    min_decay = math.log(HY_TARGET) / HY_SLOW_DECAY
    deltas = jnp.abs(jnp.linspace(min_decay, max_decay, D_MODEL, dtype=f32))
    window = jnp.exp(-t * deltas) + HY_SHIFT
    h = h * window[:, None, None, :]
    filt = jnp.concatenate([h[:, 0], jnp.zeros((1, HY_ORDER, D_MODEL), f32), h[:0:-1, 1]], axis=0)
    filt = filt / jnp.sum(jnp.abs(filt), axis=0, keepdims=True)
    return jnp.fft.rfft(filt, axis=0)


def long_conv(u, filt_f, skip):
    L = u.shape[1]
    uf = jnp.fft.rfft(u.astype(jnp.float32), n=2 * L, axis=1)
    y = jnp.fft.irfft(uf * filt_f[None], n=2 * L, axis=1)[:, :L]
    return (y + u.astype(jnp.float32) * skip.astype(jnp.float32)).astype(u.dtype)


def hyena_mix(h, w_in, b_in, conv_w, conv_b, filt_params, skip, w_out, b_out):
    L = h.shape[1]
    proj = short_conv(h @ w_in + b_in, conv_w, conv_b)
    v, x1, x2 = jnp.split(proj, 3, axis=-1)
    filt_f = hyena_filters(L, *filt_params)
    z = x1 * long_conv(v, filt_f[:, 0], skip[0])
    z = x2 * long_conv(z, filt_f[:, 1], skip[1])
    return z @ w_out + b_out


def na_mix(hx, hc, w_qkv, q_g, k_g, rpb, w_out, with_ctx_out):
    B, L, _ = hx.shape
    rows = L // GRID_W
    kh = min(WIN_H, rows)
    scale = NA_HEAD_DIM ** -0.5
    w_q, w_k, w_v = jnp.split(w_qkv, 3, axis=1)

    def split_heads(t):
        return t.reshape(t.shape[:2] + (NA_HEADS, NA_HEAD_DIM))

    def queries(h):
        return rmsnorm(split_heads(h @ w_q), q_g) * scale

    def keys(h):
        return rmsnorm(split_heads(h @ w_k), k_g)

    def values(h):
        return split_heads(h @ w_v)

    kc, vc = keys(hc), values(hc)
    grid = (B, rows, GRID_W, NA_HEADS, NA_HEAD_DIM)
    qg = queries(hx).reshape(grid)
    kg = keys(hx).reshape(grid)
    vg = values(hx).reshape(grid)
    cols = jnp.arange(GRID_W)
    c_start = jnp.clip(cols - WIN_W // 2, 0, GRID_W - WIN_W)
    col_in = (cols[None, :] >= c_start[:, None]) & (cols[None, :] < c_start[:, None] + WIN_W)
    col_idx = jnp.clip(cols[None, :] - cols[:, None], -(WIN_W - 1), WIN_W - 1) + (WIN_W - 1)
    rpb_cols = rpb.astype(jnp.float32)[:, :, col_idx]
    n_loc = kh * GRID_W

    def row_step(r):
        r0 = jnp.clip(r - kh // 2, 0, rows - kh)
        q_r = lax.dynamic_index_in_dim(qg, r, axis=1, keepdims=False)
        k_b = lax.dynamic_slice_in_dim(kg, r0, kh, axis=1)
        v_b = lax.dynamic_slice_in_dim(vg, r0, kh, axis=1)
        row_idx = r0 + jnp.arange(kh) - r + (WIN_H - 1)
        bias = jnp.transpose(rpb_cols[:, row_idx], (0, 2, 1, 3))
        s_loc = jnp.einsum('bqhd,brkhd->bhqrk', q_r, k_b, preferred_element_type=jnp.float32) + bias[None]
        s_loc = jnp.where(col_in[None, None, :, None, :], s_loc, -jnp.inf)
        s_ctx = jnp.einsum('bqhd,bchd->bhqc', q_r, kc, preferred_element_type=jnp.float32)
        s = jnp.concatenate([s_loc.reshape(B, NA_HEADS, GRID_W, n_loc), s_ctx], axis=-1)
        p = jax.nn.softmax(s, axis=-1).astype(vg.dtype)
        p_loc = p[..., :n_loc].reshape(B, NA_HEADS, GRID_W, kh, GRID_W)
        return (jnp.einsum('bhqrk,brkhd->bqhd', p_loc, v_b)
                + jnp.einsum('bhqc,bchd->bqhd', p[..., n_loc:], vc))

    o = lax.map(row_step, jnp.arange(rows))
    yx = jnp.moveaxis(o, 0, 1).reshape(B, L, D_MODEL) @ w_out
    yc = None
    if with_ctx_out:
        qc = queries(hc)
        s = jnp.einsum('bqhd,bkhd->bhqk', qc, kc, preferred_element_type=jnp.float32)
        p = jax.nn.softmax(s, axis=-1).astype(vc.dtype)
        yc = jnp.einsum('bhqk,bkhd->bqhd', p, vc).reshape(hc.shape) @ w_out
    return yx, yc


def peer(h, w_q, k1, k2, u_tab, v_tab):
    B, L, D = h.shape
    half = PEER_DKEY // 2
    n_cand = PEER_TOPK * PEER_TOPK

    def block(xb):
        q = (xb @ w_q).reshape(PEER_BLOCK, PEER_HEADS, 2, half)
        s1 = jnp.einsum('thd,hnd->thn', q[:, :, 0], k1, preferred_element_type=jnp.float32)
        s2 = jnp.einsum('thd,hnd->thn', q[:, :, 1], k2, preferred_element_type=jnp.float32)
        v1, i1 = lax.top_k(s1, PEER_TOPK)
        v2, i2 = lax.top_k(s2, PEER_TOPK)
        cand = (v1[..., :, None] + v2[..., None, :]).reshape(PEER_BLOCK, PEER_HEADS, n_cand)
        cand_id = (i1[..., :, None] * PEER_KEYS + i2[..., None, :]).reshape(PEER_BLOCK, PEER_HEADS, n_cand)
        score, pos = lax.top_k(cand, PEER_TOPK)
        eid = jnp.take_along_axis(cand_id, pos, axis=-1)
        g = jax.nn.softmax(score, axis=-1)
        act = jax.nn.gelu(jnp.einsum('td,thkd->thk', xb, u_tab[eid], preferred_element_type=jnp.float32), approximate=False)
        return jnp.einsum('thk,thkd->td', (g * act).astype(v_tab.dtype), v_tab[eid])

    out = lax.map(block, h.reshape(B * L // PEER_BLOCK, PEER_BLOCK, D))
    return out.reshape(B, L, D)


def setup_inputs(seed: int = 0) -> dict:
    key = jax.random.key(seed)
    ks = iter(jax.random.split(key, 40))

    def nrm(shape, scale):
        return jax.random.normal(next(ks), shape, jnp.float32) * scale

    D = D_MODEL
    dinv = D ** -0.5
    return {
        'x': nrm((BATCH, SEQ, D), 1.0),
        'c': nrm((BATCH, D), 1.0),
        'ctx': nrm((BATCH, CTX_LEN, D), 1.0),
        'c_ctx': nrm((D,), 1.0),
        'ada_w': nrm((DEPTH, D, 6 * D), 0.5 * dinv),
        'ada_b': nrm((DEPTH, 6 * D), 0.01),
        'norm1_g': 1.0 + nrm((DEPTH, D), 0.01),
        'norm2_g': 1.0 + nrm((DEPTH, D), 0.01),
        'hy_w_in': nrm((N_HY, D, 3 * D), dinv),
        'hy_b_in': nrm((N_HY, 3 * D), 0.01),
        'hy_conv_w': nrm((N_HY, HY_SHORT, 3 * D), 0.5),
        'hy_conv_b': nrm((N_HY, 3 * D), 0.01),
        'hy_f_w1': nrm((N_HY, HY_EMB, HY_FILT), HY_EMB ** -0.5),
        'hy_f_b1': nrm((N_HY, HY_FILT), 0.1),
        'hy_f_w2': nrm((N_HY, HY_FILT, HY_FILT), HY_FILT ** -0.5),
        'hy_f_b2': nrm((N_HY, HY_FILT), 0.1),
        'hy_f_w3': nrm((N_HY, HY_FILT, HY_FILT), HY_FILT ** -0.5),
        'hy_f_b3': nrm((N_HY, HY_FILT), 0.1),
        'hy_f_w4': nrm((N_HY, HY_FILT, 2 * HY_ORDER * D), HY_FILT ** -0.5),
        'hy_f_freq': 1.0 + nrm((N_HY, HY_FILT), 0.01),
        'hy_skip': nrm((N_HY, HY_ORDER, D), 1.0),
        'hy_w_out': nrm((N_HY, D, D), dinv),
        'hy_b_out': nrm((N_HY, D), 0.01),
        'na_w_qkv': nrm((N_NA, D, 3 * D), dinv),
        'na_q_g': 1.0 + nrm((N_NA, NA_HEAD_DIM), 0.01),
        'na_k_g': 1.0 + nrm((N_NA, NA_HEAD_DIM), 0.01),
        'na_rpb': nrm((N_NA, NA_HEADS, 2 * WIN_H - 1, 2 * WIN_W - 1), 0.1),
        'na_w_out': nrm((N_NA, D, D), dinv),
        'peer_wq': nrm((DEPTH, D, PEER_HEADS * PEER_DKEY), dinv),
        'peer_k1': nrm((DEPTH, PEER_HEADS, PEER_KEYS, PEER_DKEY // 2), (PEER_DKEY // 2) ** -0.5),
        'peer_k2': nrm((DEPTH, PEER_HEADS, PEER_KEYS, PEER_DKEY // 2), (PEER_DKEY // 2) ** -0.5),
        'peer_u': nrm((DEPTH, PEER_EXPERTS, D), dinv),
        'peer_v': nrm((DEPTH, PEER_EXPERTS, D), 0.5),
    }


def reference(x, c, ctx, c_ctx, ada_w, ada_b, norm1_g, norm2_g,
              hy_w_in, hy_b_in, hy_conv_w, hy_conv_b, hy_f_w1, hy_f_b1, hy_f_w2, hy_f_b2,
              hy_f_w3, hy_f_b3, hy_f_w4, hy_f_freq, hy_skip, hy_w_out, hy_b_out,
              na_w_qkv, na_q_g, na_k_g, na_rpb, na_w_out,
              peer_wq, peer_k1, peer_k2, peer_u, peer_v):
    hx, hc = x, ctx
    for i in range(DEPTH):
        last = i == DEPTH - 1
        j = i // N_MIXERS
        mx = ada_mod(c, ada_w[i], ada_b[i])
        mc = ada_mod(c_ctx, ada_w[i], ada_b[i])
        hx_n = modulate(rmsnorm(hx, norm1_g[i]), mx[0], mx[1])
        if i % N_MIXERS == 0:
            filt_params = (hy_f_w1[j], hy_f_b1[j], hy_f_w2[j], hy_f_b2[j],
                           hy_f_w3[j], hy_f_b3[j], hy_f_w4[j], hy_f_freq[j])
            hy_args = (hy_w_in[j], hy_b_in[j], hy_conv_w[j], hy_conv_b[j], filt_params,
                       hy_skip[j], hy_w_out[j], hy_b_out[j])
            yx = hyena_mix(hx_n, *hy_args)
            yc = None
            if not last:
                yc = hyena_mix(modulate(rmsnorm(hc, norm1_g[i]), mc[0], mc[1]), *hy_args)
        else:
            hc_n = modulate(rmsnorm(hc, norm1_g[i]), mc[0], mc[1])
            yx, yc = na_mix(hx_n, hc_n, na_w_qkv[j], na_q_g[j], na_k_g[j], na_rpb[j], na_w_out[j],
                            not last)
        peer_args = (peer_wq[i], peer_k1[i], peer_k2[i], peer_u[i], peer_v[i])
        hx = hx + mx[2] * yx
        hx = hx + mx[5] * peer(modulate(rmsnorm(hx, norm2_g[i]), mx[3], mx[4]), *peer_args)
        if not last:
            hc = hc + mc[2] * yc
            hc = hc + mc[5] * peer(modulate(rmsnorm(hc, norm2_g[i]), mc[3], mc[4]), *peer_args)
    return hx
```

```python
import functools
import math

import jax
import jax.numpy as jnp
from jax import lax
from jax.experimental import pallas as pl
from jax.experimental.pallas import tpu as pltpu

F32 = jnp.float32
BF16 = jnp.bfloat16

GRID_W = 64
N_MIXERS = 2
HY_ORDER = 2
HY_BANDS = 16
HY_FAST_DECAY = 0.3
HY_SLOW_DECAY = 1.5
HY_TARGET = 1e-2
HY_SHIFT = 0.05
NA_HEAD_DIM = 128
WIN_H = 8
WIN_W = 16
PEER_HEADS = 8
PEER_KEYS = 128
PEER_DKEY = 256
PEER_TOPK = 16
EPS = 1e-6

V7X_VMEM_BYTES = 64 * 2**20
VMEM_LIMIT_BYTES = V7X_VMEM_BYTES - 8 * 2**20
LANES = 128
NEG_BIG = -1e30
INV_SQRT2 = 0.7071067811865476


def _tile(dim, want, align):
    if dim <= want:
        return dim
    t = want - want % align
    while t > align and dim % t:
        t -= align
    assert dim % t == 0, (dim, want, align)
    return t


def _params(*semantics):
    return pltpu.CompilerParams(dimension_semantics=semantics, vmem_limit_bytes=VMEM_LIMIT_BYTES)


def _mm_body(*refs, nk, has_bias, has_res):
    a_ref, b_ref = refs[0], refs[1]
    pos = 2
    bias_ref = res_ref = gate_ref = None
    if has_bias:
        bias_ref = refs[pos]
        pos += 1
    if has_res:
        res_ref, gate_ref = refs[pos], refs[pos + 1]
        pos += 2
    o_ref, acc_ref = refs[pos], refs[pos + 1]
    k = pl.program_id(2)

    @pl.when(k == 0)
    def _init():
        acc_ref[...] = jnp.zeros_like(acc_ref)

    acc_ref[...] += jnp.dot(a_ref[...].astype(BF16), b_ref[...].astype(BF16), preferred_element_type=F32)

    @pl.when(k == nk - 1)
    def _finish():
        y = acc_ref[...]
        if has_bias:
            y = y + bias_ref[...]
        if has_res:
            y = res_ref[...] + gate_ref[0] * y
        o_ref[...] = y.astype(o_ref.dtype)


def _matmul(a, b, *, name, bias=None, res=None, gate=None, rows_per_gate=None, b_col0=0, n_out=None,
            out_dtype=F32, tm=1024, tn=1024, tk=1024):
    m, kdim = a.shape
    n = b.shape[1] if n_out is None else n_out
    tm = _tile(rows_per_gate or m, tm, 8)
    tn, tk = _tile(math.gcd(n, b_col0) if b_col0 else n, tn, LANES), _tile(kdim, tk, LANES)
    assert m % tm == 0 and n % tn == 0 and kdim % tk == 0 and b_col0 % tn == 0
    col0 = b_col0 // tn
    nk = kdim // tk
    in_specs = [pl.BlockSpec((tm, tk), lambda i, j, k: (i, k)),
                pl.BlockSpec((tk, tn), lambda i, j, k: (k, j + col0))]
    args = [a, b]
    if bias is not None:
        in_specs.append(pl.BlockSpec((1, tn), lambda i, j, k: (0, j)))
        args.append(bias.reshape(1, n))
    if res is not None:
        assert rows_per_gate % tm == 0
        tiles_per_gate = rows_per_gate // tm
        in_specs.append(pl.BlockSpec((tm, tn), lambda i, j, k: (i, j)))
        in_specs.append(pl.BlockSpec((1, 1, tn), lambda i, j, k: (i // tiles_per_gate, 0, j)))
        args += [res, gate]
    return pl.pallas_call(
        functools.partial(_mm_body, nk=nk, has_bias=bias is not None, has_res=res is not None),
        grid=(m // tm, n // tn, nk),
        in_specs=in_specs,
        out_specs=pl.BlockSpec((tm, tn), lambda i, j, k: (i, j)),
        out_shape=jax.ShapeDtypeStruct((m, n), out_dtype),
        scratch_shapes=[pltpu.VMEM((tm, tn), F32)],
        compiler_params=_params("parallel", "parallel", "arbitrary"),
        name=name,
    )(*args)


def _normmod_body(x_ref, g_ref, shift_ref, scale_ref, o_ref):
    x = x_ref[...]
    ms = jnp.mean(x * x, axis=-1, keepdims=True)
    y = x * lax.rsqrt(ms + EPS) * g_ref[...]
    o_ref[...] = (y * (1.0 + scale_ref[0]) + shift_ref[0]).astype(o_ref.dtype)


def _normmod(x, g, shift, scale, rows_per_mod, *, name, tm=256):
    m, d = x.shape
    tm = min(tm, rows_per_mod)
    assert m % tm == 0 and rows_per_mod % tm == 0
    tiles_per_mod = rows_per_mod // tm
    mod_spec = pl.BlockSpec((1, 1, d), lambda i: (i // tiles_per_mod, 0, 0))
    return pl.pallas_call(
        _normmod_body,
        grid=(m // tm,),
        in_specs=[pl.BlockSpec((tm, d), lambda i: (i, 0)), pl.BlockSpec((1, d), lambda i: (0, 0)),
                  mod_spec, mod_spec],
        out_specs=pl.BlockSpec((tm, d), lambda i: (i, 0)),
        out_shape=jax.ShapeDtypeStruct((m, d), BF16),
        compiler_params=_params("parallel"),
        name=name,
    )(x, g.reshape(1, d), shift, scale)


def _sconv_body(x_ref, w_ref, b_ref, o_ref):
    u = x_ref[0]
    length = u.shape[0]
    rows = lax.broadcasted_iota(jnp.int32, u.shape, 0)
    prev = jnp.where(rows == 0, 0.0, pltpu.roll(u, 1, 0))
    nxt = jnp.where(rows == length - 1, 0.0, pltpu.roll(u, length - 1, 0))
    o_ref[0] = b_ref[...] + prev * w_ref[0:1, :] + u * w_ref[1:2, :] + nxt * w_ref[2:3, :]


def _short_conv(proj, w, b, *, name, tc=512):
    nb, length, c = proj.shape
    tc = min(tc, c)
    return pl.pallas_call(
        _sconv_body,
        grid=(nb, c // tc),
        in_specs=[pl.BlockSpec((1, length, tc), lambda i, j: (i, 0, j)),
                  pl.BlockSpec((3, tc), lambda i, j: (0, j)),
                  pl.BlockSpec((1, tc), lambda i, j: (0, j))],
        out_specs=pl.BlockSpec((1, length, tc), lambda i, j: (i, 0, j)),
        out_shape=jax.ShapeDtypeStruct(proj.shape, F32),
        compiler_params=_params("parallel", "parallel"),
        name=name,
    )(proj, w, b.reshape(1, c))


def _filter_mlp(length, w1, b1, w2, b2, w3, b3, freq):
    hp = lax.Precision.HIGHEST
    t = jnp.linspace(0.0, 1.0, length, dtype=F32)[:, None]
    wpos = (2.0 * math.pi / length) * jnp.arange(length, dtype=F32)[:, None]
    bands = jnp.linspace(1e-4, HY_BANDS - 1, HY_BANDS, dtype=F32)[None, :]
    z = jnp.concatenate([t, jnp.cos(bands * wpos), -jnp.sin(bands * wpos)], axis=-1)
    h = jnp.sin(freq * (jnp.dot(z, w1, precision=hp) + b1))
    h = jnp.sin(freq * (jnp.dot(h, w2, precision=hp) + b2))
    return jnp.sin(freq * (jnp.dot(h, w3, precision=hp) + b3))


def _filt_body(h3_ref, wf_ref, wb_ref, fw_ref, bw_ref, *, d_model):
    j = pl.program_id(1)
    h3 = h3_ref[...].astype(BF16)
    hf = jnp.dot(h3, wf_ref[...].astype(BF16), preferred_element_type=F32)
    hb = jnp.dot(h3, wb_ref[...].astype(BF16), preferred_element_type=F32)
    length, tc = hf.shape
    rows = lax.broadcasted_iota(jnp.int32, hf.shape, 0)
    t = rows.astype(F32) * (1.0 / (length - 1))
    chan = (lax.broadcasted_iota(jnp.int32, (1, tc), 1) + j * tc).astype(F32)
    max_decay = math.log(HY_TARGET) / HY_FAST_DECAY
    min_decay = math.log(HY_TARGET) / HY_SLOW_DECAY
    deltas = jnp.abs(min_decay + chan * ((max_decay - min_decay) / (d_model - 1)))
    window = jnp.exp(-t * deltas) + HY_SHIFT
    fwd = hf * window
    bwd = jnp.where(rows == 0, 0.0, hb * window)
    norm = jnp.sum(jnp.abs(fwd), axis=0, keepdims=True) + jnp.sum(jnp.abs(bwd), axis=0, keepdims=True)
    fw_ref[0] = fwd / norm
    bw_ref[0] = bwd / norm


def _hyena_filters_time(h3, w4, d_model, *, name, tc=512):
    length, nf = h3.shape
    tc = min(tc, d_model)
    nj = d_model // tc
    return pl.pallas_call(
        functools.partial(_filt_body, d_model=d_model),
        grid=(HY_ORDER, nj),
        in_specs=[pl.BlockSpec((length, nf), lambda o, j: (0, 0)),
                  pl.BlockSpec((nf, tc), lambda o, j: (0, o * nj + j)),
                  pl.BlockSpec((nf, tc), lambda o, j: (0, (HY_ORDER + o) * nj + j))],
        out_specs=[pl.BlockSpec((1, length, tc), lambda o, j: (o, 0, j))] * 2,
        out_shape=[jax.ShapeDtypeStruct((HY_ORDER, length, d_model), F32)] * 2,
        compiler_params=_params("parallel", "parallel"),
        name=name,
    )(h3, w4, w4)


def _dft_mats(length):
    n_fft = 2 * length
    k = lax.broadcasted_iota(jnp.int32, (length, length), 0)
    n = lax.broadcasted_iota(jnp.int32, (length, length), 1)
    ang = ((k * n) % n_fft).astype(F32) * (2.0 * math.pi / n_fft)
    pc = jnp.cos(ang)
    alt = jnp.where(n % 2 == 0, 1.0, -1.0)
    ps = jnp.where(k == 0, alt, jnp.sin(ang))
    colscale = jnp.where(n == 0, 1.0 / n_fft, 2.0 / n_fft)
    qc = pc.T * colscale
    qs = ps.T * colscale
    return pc.astype(BF16), ps.astype(BF16), qc.astype(BF16), qs.astype(BF16)


def _fspec_body(pc_ref, ps_ref, fw_ref, bw_ref, hr_ref, hs_ref):
    f = pl.program_id(2)
    fw, bw = fw_ref[0], bw_ref[0]
    tot = fw + bw
    dif = fw - bw
    hr_ref[0] = jnp.dot(pc_ref[...], tot.astype(BF16), preferred_element_type=F32)
    hs = jnp.dot(ps_ref[...], dif.astype(BF16), preferred_element_type=F32)
    n = lax.broadcasted_iota(jnp.int32, tot.shape, 0)
    nyq = jnp.sum(jnp.where(n % 2 == 0, tot, -tot), axis=0, keepdims=True)
    krow = lax.broadcasted_iota(jnp.int32, hs.shape, 0)
    hs_ref[0] = jnp.where((krow == 0) & (f == 0), nyq, hs)


def _filter_spectrum(pc, ps, fw, bw, *, name, tc=512, tf=512):
    orders, length, d = fw.shape
    tc, tf = min(tc, d), min(tf, length)
    p_spec = pl.BlockSpec((tf, length), lambda o, j, f: (f, 0))
    x_spec = pl.BlockSpec((1, length, tc), lambda o, j, f: (o, 0, j))
    h_spec = pl.BlockSpec((1, tf, tc), lambda o, j, f: (o, f, j))
    return pl.pallas_call(
        _fspec_body,
        grid=(orders, d // tc, length // tf),
        in_specs=[p_spec, p_spec, x_spec, x_spec],
        out_specs=[h_spec, h_spec],
        out_shape=[jax.ShapeDtypeStruct((orders, length, d), F32)] * 2,
        compiler_params=_params("parallel", "parallel", "arbitrary"),
        name=name,
    )(pc, ps, fw, bw)


def _cfwd_body(pc_ref, ps_ref, u_ref, hr_ref, hs_ref, yr_ref, ys_ref):
    f = pl.program_id(2)
    u = u_ref[0].astype(BF16)
    xr = jnp.dot(pc_ref[...], u, preferred_element_type=F32)
    xs = jnp.dot(ps_ref[...], u, preferred_element_type=F32)
    hr, hs = hr_ref[0], hs_ref[0]
    packed = (lax.broadcasted_iota(jnp.int32, xr.shape, 0) == 0) & (f == 0)
    yr_ref[0] = jnp.where(packed, xr * hr, xr * hr - xs * hs).astype(yr_ref.dtype)
    ys_ref[0] = jnp.where(packed, xs * hs, xr * hs + xs * hr).astype(ys_ref.dtype)


def _conv_spectrum(pc, ps, u, u_col0, hr, hs, order, d, *, name, tc=512, tf=512):
    nb, length, _ = u.shape
    tc, tf = min(tc, d), min(tf, length)
    ucol = u_col0 // tc
    p_spec = pl.BlockSpec((tf, length), lambda j, b, f: (f, 0))
    h_spec = pl.BlockSpec((1, tf, tc), lambda j, b, f: (order, f, j))
    y_spec = pl.BlockSpec((1, tf, tc), lambda j, b, f: (b, f, j))
    return pl.pallas_call(
        _cfwd_body,
        grid=(d // tc, nb, length // tf),
        in_specs=[p_spec, p_spec, pl.BlockSpec((1, length, tc), lambda j, b, f: (b, 0, ucol + j)),
                  h_spec, h_spec],
        out_specs=[y_spec, y_spec],
        out_shape=[jax.ShapeDtypeStruct((nb, length, d), BF16)] * 2,
        compiler_params=_params("parallel", "parallel", "arbitrary"),
        name=name,
    )(pc, ps, u, hr, hs)


def _cinv_body(qc_ref, qs_ref, yr_ref, ys_ref, u_ref, skip_ref, gate_ref, o_ref):
    y = jnp.dot(qc_ref[...], yr_ref[0], preferred_element_type=F32)
    y = y + jnp.dot(qs_ref[...], ys_ref[0], preferred_element_type=F32)
    o_ref[0] = (gate_ref[0] * (y + u_ref[0] * skip_ref[0])).astype(o_ref.dtype)


def _conv_inverse(qc, qs, yr, ys, u, u_col0, skip, order, gate, gate_col0, *, name, out_dtype, tc=512, tt=512):
    nb, length, d = yr.shape
    tc, tt = min(tc, d), min(tt, length)
    ucol, gcol = u_col0 // tc, gate_col0 // tc
    q_spec = pl.BlockSpec((tt, length), lambda j, b, t: (t, 0))
    y_spec = pl.BlockSpec((1, length, tc), lambda j, b, t: (b, 0, j))
    return pl.pallas_call(
        _cinv_body,
        grid=(d // tc, nb, length // tt),
        in_specs=[q_spec, q_spec, y_spec, y_spec,
                  pl.BlockSpec((1, tt, tc), lambda j, b, t: (b, t, ucol + j)),
                  pl.BlockSpec((1, 1, tc), lambda j, b, t: (order, 0, j)),
                  pl.BlockSpec((1, tt, tc), lambda j, b, t: (b, t, gcol + j))],
        out_specs=pl.BlockSpec((1, tt, tc), lambda j, b, t: (b, t, j)),
        out_shape=jax.ShapeDtypeStruct((nb, length, d), out_dtype),
        compiler_params=_params("parallel", "parallel", "arbitrary"),
        name=name,
    )(qc, qs, yr, ys, u, skip.reshape(HY_ORDER, 1, d), gate)


def _hyena_mix(h, nb, length, hy, res, gate, *, tag):
    d = h.shape[1]
    proj = _matmul(h, hy["w_in"], bias=hy["b_in"], name=f"hy_in_{tag}")
    proj = _short_conv(proj.reshape(nb, length, 3 * d), hy["conv_w"], hy["conv_b"], name=f"hy_sconv_{tag}")
    h3 = _filter_mlp(length, *hy["mlp"])
    fw, bw = _hyena_filters_time(h3, hy["w4"], d, name=f"hy_filt_{tag}")
    pc, ps, qc, qs = _dft_mats(length)
    hr, hs = _filter_spectrum(pc, ps, fw, bw, name=f"hy_fspec_{tag}")
    yr, ys = _conv_spectrum(pc, ps, proj, 0, hr, hs, 0, d, name=f"hy_cfwd0_{tag}")
    z = _conv_inverse(qc, qs, yr, ys, proj, 0, hy["skip"], 0, proj, d, name=f"hy_cinv0_{tag}", out_dtype=F32)
    yr, ys = _conv_spectrum(pc, ps, z, 0, hr, hs, 1, d, name=f"hy_cfwd1_{tag}")
    z = _conv_inverse(qc, qs, yr, ys, z, 0, hy["skip"], 1, proj, 2 * d, name=f"hy_cinv1_{tag}", out_dtype=BF16)
    return _matmul(z.reshape(nb * length, d), hy["w_out"], bias=hy["b_out"], res=res, gate=gate,
                   rows_per_gate=length if gate.shape[0] > 1 else nb * length, name=f"hy_out_{tag}")


def _qknorm_body(x_ref, g_ref, o_ref, *, scale, first_part):
    part = pl.program_id(1) + first_part

    @pl.when(part < 2)
    def _normed():
        post = jnp.where(part == 0, scale, 1.0)
        for h in range(x_ref.shape[1] // NA_HEAD_DIM):
            sl = slice(h * NA_HEAD_DIM, (h + 1) * NA_HEAD_DIM)
            x = x_ref[:, sl]
            ms = jnp.mean(x * x, axis=-1, keepdims=True)
            o_ref[:, sl] = (x * lax.rsqrt(ms + EPS) * g_ref[0] * post).astype(o_ref.dtype)

    @pl.when(part == 2)
    def _plain():
        o_ref[...] = x_ref[...].astype(o_ref.dtype)


def _qk_norm(x, gains, d, first_part, *, name, tm=256, tc=1024):
    m, n = x.shape
    tm, tc = min(tm, m), min(tc, d)
    nparts = n // d
    nc = d // tc
    return pl.pallas_call(
        functools.partial(_qknorm_body, scale=NA_HEAD_DIM ** -0.5, first_part=first_part),
        grid=(m // tm, nparts, nc),
        in_specs=[pl.BlockSpec((tm, tc), lambda i, p, c: (i, p * nc + c)),
                  pl.BlockSpec((1, 1, NA_HEAD_DIM), lambda i, p, c: (p + first_part, 0, 0))],
        out_specs=pl.BlockSpec((tm, tc), lambda i, p, c: (i, p * nc + c)),
        out_shape=jax.ShapeDtypeStruct((m, n), BF16),
        compiler_params=_params("parallel", "parallel", "parallel"),
        name=name,
    )(x, gains)


def _na_bias_table(rpb, kh):
    cols = jnp.arange(GRID_W)
    c_start = jnp.clip(cols - WIN_W // 2, 0, GRID_W - WIN_W)
    col_in = (cols[None, :] >= c_start[:, None]) & (cols[None, :] < c_start[:, None] + WIN_W)
    col_idx = jnp.clip(cols[None, :] - cols[:, None], -(WIN_W - 1), WIN_W - 1) + (WIN_W - 1)
    rpb_cols = rpb.astype(F32)[:, :, col_idx]
    tabs = []
    for s in range(2 * WIN_H - kh):
        b = jnp.transpose(rpb_cols[:, s:s + kh], (0, 2, 1, 3))
        b = jnp.where(col_in[None, :, None, :], b, NEG_BIG)
        tabs.append(b.reshape(b.shape[0], GRID_W, kh * GRID_W))
    return jnp.stack(tabs)


def _na_body(q_ref, k_ref, v_ref, kc_ref, vc_ref, bias_ref, o_ref, *, rows, kh, heads):
    r = pl.program_id(2)
    r0 = jnp.clip(r - kh // 2, 0, rows - kh)
    start = pl.multiple_of(r0 * GRID_W, GRID_W)
    nt = (((1,), (1,)), ((), ()))
    for h in range(heads):
        sl = slice(h * NA_HEAD_DIM, (h + 1) * NA_HEAD_DIM)
        q = q_ref[0, :, sl]
        kl = k_ref[0, pl.ds(start, kh * GRID_W), sl]
        vl = v_ref[0, pl.ds(start, kh * GRID_W), sl]
        s_loc = lax.dot_general(q, kl, nt, preferred_element_type=F32) + bias_ref[0, h]
        s_ctx = lax.dot_general(q, kc_ref[0, :, sl], nt, preferred_element_type=F32)
        m = jnp.maximum(jnp.max(s_loc, axis=-1, keepdims=True), jnp.max(s_ctx, axis=-1, keepdims=True))
        p_loc = jnp.exp(s_loc - m)
        p_ctx = jnp.exp(s_ctx - m)
        denom = jnp.sum(p_loc, axis=-1, keepdims=True) + jnp.sum(p_ctx, axis=-1, keepdims=True)
        o = jnp.dot(p_loc.astype(BF16), vl, preferred_element_type=F32)
        o = o + jnp.dot(p_ctx.astype(BF16), vc_ref[0, :, sl], preferred_element_type=F32)
        o_ref[0, :, sl] = (o / denom).astype(o_ref.dtype)


def _na_attention(qkv, kvc, bias_tab, nb, length, ctx_len, d, *, name, heads=8):
    rows = length // GRID_W
    kh = min(WIN_H, rows)
    heads = min(heads, d // NA_HEAD_DIM)
    tc = heads * NA_HEAD_DIM
    ng = d // tc
    qkv = qkv.reshape(nb, length, 3 * d)
    kvc = kvc.reshape(nb, ctx_len, 2 * d)

    def bias_index(b, g, r):
        r0 = jnp.clip(r - kh // 2, 0, rows - kh)
        return (r0 - r + WIN_H - 1, g, 0, 0)

    return pl.pallas_call(
        functools.partial(_na_body, rows=rows, kh=kh, heads=heads),
        grid=(nb, ng, rows),
        in_specs=[pl.BlockSpec((1, GRID_W, tc), lambda b, g, r: (b, r, g)),
                  pl.BlockSpec((1, length, tc), lambda b, g, r: (b, 0, ng + g)),
                  pl.BlockSpec((1, length, tc), lambda b, g, r: (b, 0, 2 * ng + g)),
                  pl.BlockSpec((1, ctx_len, tc), lambda b, g, r: (b, 0, g)),
                  pl.BlockSpec((1, ctx_len, tc), lambda b, g, r: (b, 0, ng + g)),
                  pl.BlockSpec((1, heads, GRID_W, kh * GRID_W), bias_index)],
        out_specs=pl.BlockSpec((1, GRID_W, tc), lambda b, g, r: (b, r, g)),
        out_shape=jax.ShapeDtypeStruct((nb, length, d), BF16),
        compiler_params=_params("parallel", "parallel", "arbitrary"),
        name=name,
    )(qkv, qkv, qkv, kvc, kvc, bias_tab)


def _peer_candidates():
    return [(a, b) for a in range(PEER_TOPK) for b in range(PEER_TOPK) if (a + 1) * (b + 1) <= PEER_TOPK]


def _route_body(q_ref, k1_ref, k2_ref, s1_ref, a1_ref, s2_ref, e2_ref, tau_ref, top1_ref, top2_ref, cand_ref):
    half = PEER_DKEY // 2
    nt = (((1,), (1,)), ((), ()))
    q1 = q_ref[:, :half].astype(BF16)
    q2 = q_ref[:, half:].astype(BF16)
    s1 = lax.dot_general(k1_ref[0].astype(BF16), q1, nt, preferred_element_type=F32)
    s2 = lax.dot_general(k2_ref[0].astype(BF16), q2, nt, preferred_element_type=F32)

    def top_values(x, out_ref):
        for r in range(PEER_TOPK):
            m = jnp.max(x, axis=0, keepdims=True)
            out_ref[r:r + 1, :] = m
            x = jnp.where(x == m, NEG_BIG, x)

    top_values(s1, top1_ref)
    top_values(s2, top2_ref)
    pairs = _peer_candidates()
    cand_ref[...] = jnp.full(cand_ref.shape, NEG_BIG, F32)
    for idx, (a, b) in enumerate(pairs):
        cand_ref[idx:idx + 1, :] = top1_ref[a:a + 1, :] + top2_ref[b:b + 1, :]
    c = cand_ref[...]
    best = top1_ref[0:1, :] + top2_ref[0:1, :]
    z = jnp.zeros_like(best)
    m = best
    for r in range(PEER_TOPK):
        m = jnp.max(c, axis=0, keepdims=True)
        z = z + jnp.exp(m - best)
        c = jnp.where(c == m, NEG_BIG, c)
    s1_ref[0] = s1
    s2_ref[0] = s2
    a1_ref[0] = jnp.exp(s1 - top1_ref[0:1, :]) / z
    e2_ref[0] = jnp.exp(s2 - top2_ref[0:1, :])
    tau_ref[0] = m


def _peer_route(q, k1, k2, *, name, tt=512):
    t, _ = q.shape
    tt = min(tt, t)
    ncand = -(-len(_peer_candidates()) // 8) * 8
    key_spec = pl.BlockSpec((1, PEER_KEYS, PEER_DKEY // 2), lambda i, h: (h, 0, 0))
    out_spec = pl.BlockSpec((1, PEER_KEYS, tt), lambda i, h: (h, 0, i))
    full = jax.ShapeDtypeStruct((PEER_HEADS, PEER_KEYS, t), F32)
    return pl.pallas_call(
        _route_body,
        grid=(t // tt, PEER_HEADS),
        in_specs=[pl.BlockSpec((tt, PEER_DKEY), lambda i, h: (i, h)), key_spec, key_spec],
        out_specs=[out_spec, out_spec, out_spec, out_spec, pl.BlockSpec((1, 1, tt), lambda i, h: (h, 0, i))],
        out_shape=[full, full, full, full, jax.ShapeDtypeStruct((PEER_HEADS, 1, t), F32)],
        scratch_shapes=[pltpu.VMEM((PEER_TOPK, tt), F32), pltpu.VMEM((PEER_TOPK, tt), F32),
                        pltpu.VMEM((ncand, tt), F32)],
        compiler_params=_params("parallel", "parallel"),
        name=name,
    )(q, k1, k2)


def _peer_dense_body(h_ref, u_ref, v_ref, s1_ref, a1_ref, s2_ref, e2_ref, tau_ref, res_ref, gate_ref, o_ref,
                     *, ne, keys_per_tile):
    e = pl.program_id(1)

    @pl.when(e == 0)
    def _init():
        o_ref[...] = jnp.zeros_like(o_ref)

    act = lax.dot_general(h_ref[...], u_ref[...], (((1,), (1,)), ((), ())), preferred_element_type=F32)
    act = 0.5 * act * (1.0 + lax.erf(act * INV_SQRT2))
    slabs = []
    for ii in range(keys_per_tile):
        i = e * keys_per_tile + ii
        w = jnp.zeros(s2_ref.shape[1:], F32)
        for h in range(PEER_HEADS):
            picked = (s1_ref[h, pl.ds(i, 1), :] + s2_ref[h]) >= tau_ref[h]
            w = w + jnp.where(picked, a1_ref[h, pl.ds(i, 1), :] * e2_ref[h], 0.0)
        slabs.append(w)
    w = jnp.concatenate(slabs, axis=0) if len(slabs) > 1 else slabs[0]
    g = (w.T * act).astype(BF16)
    o_ref[...] += jnp.dot(g, v_ref[...], preferred_element_type=F32)

    @pl.when(e == ne - 1)
    def _finish():
        o_ref[...] = res_ref[...] + gate_ref[0] * o_ref[...]


def _peer_dense(h, u, v, route, res, gate, rows_per_gate, *, name, tt=512, te=256):
    t, d = h.shape
    n_exp = u.shape[0]
    tt = min(tt, t)
    assert t % tt == 0 and n_exp % te == 0 and te % PEER_KEYS == 0 and rows_per_gate % tt == 0
    ne = n_exp // te
    tiles_per_gate = rows_per_gate // tt
    once = pl.Buffered(1)
    r_spec = pl.BlockSpec((PEER_HEADS, PEER_KEYS, tt), lambda i, e: (0, 0, i), pipeline_mode=once)
    s1, a1, s2, e2, tau = route
    return pl.pallas_call(
        functools.partial(_peer_dense_body, ne=ne, keys_per_tile=te // PEER_KEYS),
        grid=(t // tt, ne),
        in_specs=[pl.BlockSpec((tt, d), lambda i, e: (i, 0), pipeline_mode=once),
                  pl.BlockSpec((te, d), lambda i, e: (e, 0)),
                  pl.BlockSpec((te, d), lambda i, e: (e, 0)),
                  r_spec, r_spec, r_spec, r_spec,
                  pl.BlockSpec((PEER_HEADS, 1, tt), lambda i, e: (0, 0, i), pipeline_mode=once),
                  pl.BlockSpec((tt, d), lambda i, e: (i, 0), pipeline_mode=once),
                  pl.BlockSpec((1, 1, d), lambda i, e: (i // tiles_per_gate, 0, 0))],
        out_specs=pl.BlockSpec((tt, d), lambda i, e: (i, 0)),
        out_shape=jax.ShapeDtypeStruct((t, d), F32),
        compiler_params=_params("parallel", "arbitrary"),
        name=name,
    )(h, u, v, s1, a1, s2, e2, tau, res, gate)


def _peer(h, pr, res, gate, rows_per_gate, *, tag):
    q = _matmul(h, pr["wq"], name=f"peer_q_{tag}")
    route = _peer_route(q, pr["k1"], pr["k2"], name=f"peer_route_{tag}")
    return _peer_dense(h, pr["u"], pr["v"], route, res, gate, rows_per_gate, name=f"peer_dense_{tag}")


def kernel(x, c, ctx, c_ctx, ada_w, ada_b, norm1_g, norm2_g, hy_w_in, hy_b_in, hy_conv_w, hy_conv_b, hy_f_w1, hy_f_b1, hy_f_w2, hy_f_b2, hy_f_w3, hy_f_b3, hy_f_w4, hy_f_freq, hy_skip, hy_w_out, hy_b_out, na_w_qkv, na_q_g, na_k_g, na_rpb, na_w_out, peer_wq, peer_k1, peer_k2, peer_u, peer_v):
    nb, length, d = x.shape
    ctx_len = ctx.shape[1]
    depth = ada_w.shape[0]
    assert depth == 2, "layer pattern implemented: Hyena layer followed by a final neighbourhood-attention layer"
    hx = x.reshape(nb * length, d)
    hc = ctx.reshape(nb * ctx_len, d)

    cond = jnp.concatenate([c, c_ctx[None, :], jnp.zeros((-(nb + 1) % 8, d), F32)], axis=0)
    cond = jax.nn.silu(cond)

    for i in range(depth):
        last = i == depth - 1
        j = i // N_MIXERS
        mods = _matmul(cond, ada_w[i], bias=ada_b[i], name=f"ada_{i}", tn=2048).reshape(cond.shape[0], 6, 1, d)
        mx = [mods[:nb, s] for s in range(6)]
        mc = [mods[nb:nb + 1, s] for s in range(6)]
        peer_params = dict(wq=peer_wq[i], k1=peer_k1[i], k2=peer_k2[i],
                           u=peer_u[i].astype(BF16), v=peer_v[i].astype(BF16))
        hx_n = _normmod(hx, norm1_g[i], mx[0], mx[1], length, name=f"norm1_x_{i}")
        if i % N_MIXERS == 0:
            hy = dict(w_in=hy_w_in[j], b_in=hy_b_in[j], conv_w=hy_conv_w[j], conv_b=hy_conv_b[j],
                      mlp=(hy_f_w1[j], hy_f_b1[j], hy_f_w2[j], hy_f_b2[j], hy_f_w3[j], hy_f_b3[j], hy_f_freq[j]),
                      w4=hy_f_w4[j], skip=hy_skip[j], w_out=hy_w_out[j], b_out=hy_b_out[j])
            hx = _hyena_mix(hx_n, nb, length, hy, hx, mx[2], tag=f"x{i}")
            if not last:
                hc_n = _normmod(hc, norm1_g[i], mc[0], mc[1], nb * ctx_len, name=f"norm1_c_{i}")
                hc = _hyena_mix(hc_n, nb, ctx_len, hy, hc, mc[2], tag=f"c{i}")
        else:
            assert last, "context output of a non-final attention layer is not needed for depth 2"
            hc_n = _normmod(hc, norm1_g[i], mc[0], mc[1], nb * ctx_len, name=f"norm1_c_{i}")
            gains = jnp.stack([na_q_g[j], na_k_g[j], jnp.ones_like(na_q_g[j])]).reshape(3, 1, NA_HEAD_DIM)
            qkv = _matmul(hx_n, na_w_qkv[j], name=f"na_qkv_x_{i}")
            qkv = _qk_norm(qkv, gains, d, 0, name=f"na_qknorm_x_{i}")
            kvc = _matmul(hc_n, na_w_qkv[j], b_col0=d, n_out=2 * d, name=f"na_kv_c_{i}")
            kvc = _qk_norm(kvc, gains, d, 1, name=f"na_qknorm_c_{i}")
            bias_tab = _na_bias_table(na_rpb[j], min(WIN_H, length // GRID_W))
            att = _na_attention(qkv, kvc, bias_tab, nb, length, ctx_len, d, name=f"na_attn_{i}")
            hx = _matmul(att.reshape(nb * length, d), na_w_out[j], res=hx, gate=mx[2], rows_per_gate=length,
                         name=f"na_out_{i}")
        hp = _normmod(hx, norm2_g[i], mx[3], mx[4], length, name=f"norm2_x_{i}")
        hx = _peer(hp, peer_params, hx, mx[5], length, tag=f"x{i}")
        if not last:
            hp = _normmod(hc, norm2_g[i], mc[3], mc[4], nb * ctx_len, name=f"norm2_c_{i}")
            hc = _peer(hp, peer_params, hc, mc[5], nb * ctx_len, tag=f"c{i}")
    return hx.reshape(nb, length, d)
```
